```python
import math
import jax, jax.numpy as jnp
from jax import lax
import numpy as np

D_MODEL = 1024
BATCH = 2
SEQ = 8192
DEPTH = 2
DEC_BATCH = 32
DEC_SEQ = 1
PAST_LEN = 16384
PAGE_SIZE = 128

N_HEADS_A = 8
HEAD_DIM_A = 64
V_DIM_A = 2 * HEAD_DIM_A
ATTN_QK_WIDTH = N_HEADS_A * 2 * HEAD_DIM_A
ATTN_V_WIDTH = N_HEADS_A * V_DIM_A
Q_BLOCK = 128
ROPE_THETA = 10000.0
N_HEADS_B = 8
KEY_DIM_B = 128
VAL_DIM_B = 128
GDN_K_WIDTH = N_HEADS_B * KEY_DIM_B
GDN_V_WIDTH = N_HEADS_B * VAL_DIM_B
GDN_CONV_CH = 2 * GDN_K_WIDTH + GDN_V_WIDTH
CONV_W = 4
GDN_CHUNK = 64
D_FF = 3584
N_EXPERTS = 8
TOP_K = 2
N_ATTN = (DEPTH + 1) // 2
N_GDN = DEPTH // 2
N_DENSE = (DEPTH + 1) // 2
N_MOE = DEPTH // 2
EPS = 1e-6
NEG_INF = -1e30
F32 = jnp.float32

kernel_name = "hybrid_diffattn_gdn_step"


def _rmsnorm(x, g):
    xf = x.astype(F32)
    y = xf * lax.rsqrt(jnp.mean(xf * xf, axis=-1, keepdims=True) + EPS)
    return (y * g.astype(F32)).astype(x.dtype)


def _l2norm(x):
    return x * lax.rsqrt(jnp.sum(x * x, axis=-1, keepdims=True) + EPS)


def _swiglu(h, w_gu, w_down):
    gu = h @ w_gu
    return (jax.nn.silu(gu[..., :D_FF]) * gu[..., D_FF:]) @ w_down


def _moe(h, w_router, b_router, w_gu, w_down):
    logits = (h @ w_router).astype(F32) + b_router.astype(F32)
    top_val, top_idx = lax.top_k(logits, TOP_K)
    gates = jax.nn.softmax(top_val, axis=-1)
    comb = jnp.einsum('btk,btke->bte', gates, jax.nn.one_hot(top_idx, N_EXPERTS, dtype=F32))
    y = jnp.zeros(h.shape, F32)
    for e in range(N_EXPERTS):
        y = y + comb[..., e:e + 1] * _swiglu(h, w_gu[e], w_down[e]).astype(F32)
    return y.astype(h.dtype)


def _rope(x, pos):
    half = HEAD_DIM_A // 2
    inv = ROPE_THETA ** (-jnp.arange(half, dtype=F32) / half)
    ang = pos.astype(F32)[:, None] * inv[None, :]
    cos = jnp.cos(ang)[None, :, None, None, :]
    sin = jnp.sin(ang)[None, :, None, None, :]
    xf = x.astype(F32)
    x1, x2 = xf[..., :half], xf[..., half:]
    return jnp.concatenate([x1 * cos - x2 * sin, x2 * cos + x1 * sin], axis=-1)


def _diff_lambda(lam_qk, lam_init):
    lq = lam_qk.astype(F32)
    return jnp.exp(jnp.sum(lq[0] * lq[1])) - jnp.exp(jnp.sum(lq[2] * lq[3])) + lam_init


def _diff_project(h, w_in, pos):
    B, T, _ = h.shape
    qkv = h @ w_in
    q = _rope(qkv[..., :ATTN_QK_WIDTH].reshape(B, T, N_HEADS_A, 2, HEAD_DIM_A), pos)
    k = _rope(qkv[..., ATTN_QK_WIDTH:2 * ATTN_QK_WIDTH].reshape(B, T, N_HEADS_A, 2, HEAD_DIM_A), pos)
    v = qkv[..., 2 * ATTN_QK_WIDTH:].reshape(B, T, N_HEADS_A, V_DIM_A)
    return q, k, v


def _diff_attn_prompt(q, k, v, lam):
    B, S = q.shape[:2]
    nblk = S // Q_BLOCK
    qb = jnp.moveaxis(q.reshape(B, nblk, Q_BLOCK, N_HEADS_A, 2, HEAD_DIM_A), 1, 0)
    kf, vf = k.astype(F32), v.astype(F32)
    kpos = jnp.arange(S)
    scale = HEAD_DIM_A ** -0.5

    def block(args):
        qi, bi = args
        qpos = bi * Q_BLOCK + jnp.arange(Q_BLOCK)
        s = jnp.einsum('bthcd,blhcd->bhctl', qi.astype(F32), kf) * scale
        s = jnp.where(kpos[None, :] <= qpos[:, None], s, NEG_INF)
        p = jax.nn.softmax(s, axis=-1)
        a = p[:, :, 0] - lam * p[:, :, 1]
        return jnp.einsum('bhtl,blhe->bthe', a, vf)

    o = lax.map(block, (qb, jnp.arange(nblk)))
    return jnp.moveaxis(o, 0, 1).reshape(B, S, N_HEADS_A, V_DIM_A)


def _diff_partial(q, k, v, mask):
    s = jnp.einsum('bthcd,blhcd->bhctl', q, k.astype(F32)) * HEAD_DIM_A ** -0.5
    if mask is not None:
        s = jnp.where(mask, s, NEG_INF)
    m = jnp.max(s, axis=-1)
    p = jnp.exp(s - m[..., None])
    return m, jnp.sum(p, axis=-1), jnp.einsum('bhctl,blhe->bhcte', p, v.astype(F32))


def _merge(a, b):
    m = jnp.maximum(a[0], b[0])
    sa, sb = jnp.exp(a[0] - m), jnp.exp(b[0] - m)
    return (m, a[1] * sa + b[1] * sb, a[2] * sa[..., None] + b[2] * sb[..., None])


def _diff_attn_sample(q, k_new, v_new, cache_k, cache_v, layer, page_table, lam):
    B, T = q.shape[:2]
    qf = q.astype(F32)

    def page_step(carry, pages):
        kp = cache_k[layer, pages].reshape(B, PAGE_SIZE, N_HEADS_A, 2, HEAD_DIM_A)
        vp = cache_v[layer, pages]
        return _merge(carry, _diff_partial(qf, kp, vp, None)), None

    init = (jnp.full((B, N_HEADS_A, 2, T), NEG_INF, F32),
            jnp.zeros((B, N_HEADS_A, 2, T), F32),
            jnp.zeros((B, N_HEADS_A, 2, T, V_DIM_A), F32))
    st, _ = lax.scan(page_step, init, page_table.T)
    causal = jnp.tril(jnp.ones((T, T), dtype=bool))
    m, l, acc = _merge(st, _diff_partial(qf, k_new, v_new, causal))
    o = acc / l[..., None]
    o = o[:, :, 0] - lam * o[:, :, 1]
    return jnp.moveaxis(o, 1, 2)


def _diff_output(o, subln, w_out, lam_init, dtype):
    B, T = o.shape[:2]
    o = _rmsnorm(o, subln) * (1.0 - lam_init)
    return o.reshape(B, T, ATTN_V_WIDTH).astype(dtype) @ w_out


def _gdn_project(h, w_in):
    proj = h @ w_in
    c0 = GDN_CONV_CH
    c1 = c0 + GDN_V_WIDTH
    c2 = c1 + N_HEADS_B
    return proj[..., :c0], proj[..., c0:c1], proj[..., c1:c2], proj[..., c2:]


def _gdn_conv(ext, w_conv):
    T = ext.shape[1] - (CONV_W - 1)
    y = ext[:, 0:T] * w_conv[0]
    for i in range(1, CONV_W):
        y = y + ext[:, i:i + T] * w_conv[i]
    return jax.nn.silu(y)


def _gdn_gates(qkv, a, b, a_log, dt_bias):
    B, T, _ = qkv.shape
    qkv = qkv.astype(F32)
    q = qkv[..., :GDN_K_WIDTH].reshape(B, T, N_HEADS_B, KEY_DIM_B)
    k = qkv[..., GDN_K_WIDTH:2 * GDN_K_WIDTH].reshape(B, T, N_HEADS_B, KEY_DIM_B)
    v = qkv[..., 2 * GDN_K_WIDTH:].reshape(B, T, N_HEADS_B, VAL_DIM_B)
    q = _l2norm(q) * KEY_DIM_B ** -0.5
    k = _l2norm(k)
    beta = jax.nn.sigmoid(b.astype(F32))
    g = -jnp.exp(a_log.astype(F32)) * jax.nn.softplus(a.astype(F32) + dt_bias.astype(F32))
    return q, k, v, g, beta


def _gdn_chunked(q, k, v, g, beta):
    B, L, H, dk = q.shape
    dv = v.shape[-1]
    C = GDN_CHUNK
    n = L // C

    def chunks(t):
        return jnp.moveaxis(t.reshape((B, n, C, H) + t.shape[3:]), 3, 1)

    q, k, v, g, beta = chunks(q), chunks(k), chunks(v), chunks(g), chunks(beta)
    G = jnp.cumsum(g, axis=-1)
    idx = jnp.arange(C)
    lower = idx[:, None] >= idx[None, :]
    strict = idx[:, None] > idx[None, :]
    decay = jnp.exp(jnp.where(lower, G[..., :, None] - G[..., None, :], NEG_INF))
    kb = k * beta[..., None]
    A = jnp.where(strict, jnp.einsum('bhnid,bhnjd->bhnij', kb, k) * decay, 0.0)
    M = A + jnp.eye(C, dtype=F32)
    rhs = jnp.concatenate([v * beta[..., None], kb * jnp.exp(G)[..., None]], axis=-1)
    sol = lax.linalg.triangular_solve(M, rhs, left_side=True, lower=True, unit_diagonal=True)
    u0, w = sol[..., :dv], sol[..., dv:]
    qk = jnp.einsum('bhnid,bhnjd->bhnij', q, k) * decay
    q_dec = q * jnp.exp(G)[..., None]
    G_last = G[..., -1]
    k_dec = k * jnp.exp(G_last[..., None] - G)[..., None]

    def step(S, xs):
        u0_i, w_i, qk_i, qd_i, kd_i, gl_i = xs
        u = u0_i - jnp.einsum('bhcd,bhde->bhce', w_i, S)
        o = jnp.einsum('bhcd,bhde->bhce', qd_i, S) + jnp.einsum('bhij,bhje->bhie', qk_i, u)
        S = S * jnp.exp(gl_i)[..., None, None] + jnp.einsum('bhcd,bhce->bhde', kd_i, u)
        return S, o

    xs = tuple(jnp.moveaxis(t, 2, 0) for t in (u0, w, qk, q_dec, k_dec, G_last))
    S, o = lax.scan(step, jnp.zeros((B, H, dk, dv), F32), xs)
    o = jnp.moveaxis(jnp.moveaxis(o, 0, 2), 1, 3).reshape(B, L, H, dv)
    return o, S


def _gdn_recurrent(q, k, v, g, beta, S0):
    def step(S, xs):
        q_t, k_t, v_t, g_t, b_t = xs
        S = S * jnp.exp(g_t)[..., None, None]
        u = b_t[..., None] * (v_t - jnp.einsum('bhd,bhde->bhe', k_t, S))
        S = S + jnp.einsum('bhd,bhe->bhde', k_t, u)
        return S, jnp.einsum('bhd,bhde->bhe', q_t, S)

    xs = tuple(jnp.moveaxis(t, 1, 0) for t in (q, k, v, g, beta))
    S, o = lax.scan(step, S0.astype(F32), xs)
    return jnp.moveaxis(o, 0, 1), S


def _gdn_output(o, z, onorm, w_out, dtype):
    B, T = o.shape[:2]
    zf = z.astype(F32).reshape(B, T, N_HEADS_B, VAL_DIM_B)
    o = _rmsnorm(o, onorm) * jax.nn.silu(zf)
    return o.reshape(B, T, GDN_V_WIDTH).astype(dtype) @ w_out


def setup_inputs(seed: int = 0) -> dict:
    key = jax.random.key(seed)
    ks = jax.random.split(key, 32)
    n_pages = PAST_LEN // PAGE_SIZE
    n_used = DEC_BATCH * n_pages
    n_pool = n_used + max(1, n_used // 4)

    def nrm(k, shape, scale):
        return jax.random.normal(k, shape, F32) * scale

    x_prompt = nrm(ks[0], (BATCH, SEQ, D_MODEL), 1.0)
    x_sample = nrm(ks[1], (DEC_BATCH, DEC_SEQ, D_MODEL), 1.0)
    cache_k = nrm(ks[2], (N_ATTN, n_pool, PAGE_SIZE, N_HEADS_A, 2 * HEAD_DIM_A), 1.0)
    cache_v = nrm(ks[3], (N_ATTN, n_pool, PAGE_SIZE, N_HEADS_A, V_DIM_A), 1.0)
    page_table = jax.random.permutation(ks[4], n_pool)[:n_used].reshape(DEC_BATCH, n_pages).astype(jnp.int32)
    state_ssm = nrm(ks[5], (N_GDN, DEC_BATCH, N_HEADS_B, KEY_DIM_B, VAL_DIM_B), 0.1)
    state_conv = nrm(ks[6], (N_GDN, DEC_BATCH, CONV_W - 1, GDN_CONV_CH), 1.0)
    norm_mix = 1.0 + nrm(ks[7], (DEPTH, D_MODEL), 0.02)
    norm_ffn = 1.0 + nrm(ks[8], (DEPTH, D_MODEL), 0.02)
    norm_final = 1.0 + nrm(ks[9], (D_MODEL,), 0.02)
    attn_w_in = nrm(ks[10], (N_ATTN, D_MODEL, 2 * ATTN_QK_WIDTH + ATTN_V_WIDTH), D_MODEL ** -0.5)
    attn_lambda = nrm(ks[11], (N_ATTN, 4, HEAD_DIM_A), 0.1)
    attn_subln = 1.0 + nrm(ks[12], (N_ATTN, V_DIM_A), 0.02)
    attn_w_out = nrm(ks[13], (N_ATTN, ATTN_V_WIDTH, D_MODEL), ATTN_V_WIDTH ** -0.5)
    gdn_w_in = nrm(ks[14], (N_GDN, D_MODEL, GDN_CONV_CH + GDN_V_WIDTH + 2 * N_HEADS_B), D_MODEL ** -0.5)
    gdn_conv = nrm(ks[15], (N_GDN, CONV_W, GDN_CONV_CH), CONV_W ** -0.5)
    gdn_a_log = jnp.log(jax.random.uniform(ks[16], (N_GDN, N_HEADS_B), F32, 1.0, 16.0))
    dt = jnp.exp(jax.random.uniform(ks[17], (N_GDN, N_HEADS_B), F32, math.log(1e-3), math.log(1e-1)))
    gdn_dt_bias = dt + jnp.log(-jnp.expm1(-dt))
    gdn_onorm = 1.0 + nrm(ks[18], (N_GDN, VAL_DIM_B), 0.02)
    gdn_w_out = nrm(ks[19], (N_GDN, GDN_V_WIDTH, D_MODEL), GDN_V_WIDTH ** -0.5)
    ffn_w_gu = nrm(ks[20], (N_DENSE, D_MODEL, 2 * D_FF), D_MODEL ** -0.5)
    ffn_w_down = nrm(ks[21], (N_DENSE, D_FF, D_MODEL), D_FF ** -0.5)
    moe_w_router = nrm(ks[22], (N_MOE, D_MODEL, N_EXPERTS), D_MODEL ** -0.5)
    moe_b_router = nrm(ks[23], (N_MOE, N_EXPERTS), 0.01)
    moe_w_gu = nrm(ks[24], (N_MOE, N_EXPERTS, D_MODEL, 2 * D_FF), D_MODEL ** -0.5)
    moe_w_down = nrm(ks[25], (N_MOE, N_EXPERTS, D_FF, D_MODEL), D_FF ** -0.5)
    return {"x_prompt": x_prompt, "x_sample": x_sample, "cache_k": cache_k, "cache_v": cache_v,
            "page_table": page_table, "state_ssm": state_ssm, "state_conv": state_conv,
            "norm_mix": norm_mix, "norm_ffn": norm_ffn, "norm_final": norm_final,
            "attn_w_in": attn_w_in, "attn_lambda": attn_lambda, "attn_subln": attn_subln,
            "attn_w_out": attn_w_out, "gdn_w_in": gdn_w_in, "gdn_conv": gdn_conv,
            "gdn_a_log": gdn_a_log, "gdn_dt_bias": gdn_dt_bias, "gdn_onorm": gdn_onorm,
            "gdn_w_out": gdn_w_out, "ffn_w_gu": ffn_w_gu, "ffn_w_down": ffn_w_down,
            "moe_w_router": moe_w_router, "moe_b_router": moe_b_router,
            "moe_w_gu": moe_w_gu, "moe_w_down": moe_w_down}


def reference(x_prompt, x_sample, cache_k, cache_v, page_table, state_ssm, state_conv,
              norm_mix, norm_ffn, norm_final, attn_w_in, attn_lambda, attn_subln, attn_w_out,
              gdn_w_in, gdn_conv, gdn_a_log, gdn_dt_bias, gdn_onorm, gdn_w_out,
              ffn_w_gu, ffn_w_down, moe_w_router, moe_b_router, moe_w_gu, moe_w_down):
    B, S, _ = x_prompt.shape
    Bd, T, _ = x_sample.shape
    past_len = page_table.shape[1] * PAGE_SIZE
    pos_p = jnp.arange(S, dtype=jnp.int32)
    pos_s = past_len + jnp.arange(T, dtype=jnp.int32)
    hp, hs = x_prompt, x_sample
    k_p, v_p, k_s, v_s = [], [], [], []
    ssm_p, conv_p, ssm_s, conv_s = [], [], [], []
    for i in range(DEPTH):
        j = i // 2
        ap = _rmsnorm(hp, norm_mix[i])
        a_s = _rmsnorm(hs, norm_mix[i])
        if i % 2 == 0:
            lam_init = 0.8 - 0.6 * math.exp(-0.3 * i)
            lam = _diff_lambda(attn_lambda[j], lam_init)
            qp, kp, vp = _diff_project(ap, attn_w_in[j], pos_p)
            qs, ks_, vs = _diff_project(a_s, attn_w_in[j], pos_s)
            op = _diff_attn_prompt(qp, kp, vp, lam)
            o_s = _diff_attn_sample(qs, ks_, vs, cache_k, cache_v, j, page_table, lam)
            hp = hp + _diff_output(op, attn_subln[j], attn_w_out[j], lam_init, hp.dtype)
            hs = hs + _diff_output(o_s, attn_subln[j], attn_w_out[j], lam_init, hs.dtype)
            k_p.append(kp.reshape(B, S, N_HEADS_A, 2 * HEAD_DIM_A).astype(cache_k.dtype))
            v_p.append(vp.astype(cache_v.dtype))
            k_s.append(ks_.reshape(Bd, T, N_HEADS_A, 2 * HEAD_DIM_A).astype(cache_k.dtype))
            v_s.append(vs.astype(cache_v.dtype))
            hp = hp + _swiglu(_rmsnorm(hp, norm_ffn[i]), ffn_w_gu[j], ffn_w_down[j])
            hs = hs + _swiglu(_rmsnorm(hs, norm_ffn[i]), ffn_w_gu[j], ffn_w_down[j])
        else:
            qkv_p, z_p, a_p, b_p = _gdn_project(ap, gdn_w_in[j])
            ext_p = jnp.pad(qkv_p, ((0, 0), (CONV_W - 1, 0), (0, 0)))
            q, k, v, g, beta = _gdn_gates(_gdn_conv(ext_p, gdn_conv[j]), a_p, b_p, gdn_a_log[j], gdn_dt_bias[j])
            o_p, S_p = _gdn_chunked(q, k, v, g, beta)
            hp = hp + _gdn_output(o_p, z_p, gdn_onorm[j], gdn_w_out[j], hp.dtype)
            qkv_s, z_s, a_sm, b_sm = _gdn_project(a_s, gdn_w_in[j])
            ext_s = jnp.concatenate([state_conv[j].astype(qkv_s.dtype), qkv_s], axis=1)
            q, k, v, g, beta = _gdn_gates(_gdn_conv(ext_s, gdn_conv[j]), a_sm, b_sm, gdn_a_log[j], gdn_dt_bias[j])
            o_sm, S_s = _gdn_recurrent(q, k, v, g, beta, state_ssm[j])
            hs = hs + _gdn_output(o_sm, z_s, gdn_onorm[j], gdn_w_out[j], hs.dtype)
            ssm_p.append(S_p.astype(state_ssm.dtype))
            conv_p.append(ext_p[:, -(CONV_W - 1):].astype(state_conv.dtype))
            ssm_s.append(S_s.astype(state_ssm.dtype))
            conv_s.append(ext_s[:, -(CONV_W - 1):].astype(state_conv.dtype))
            hp = hp + _moe(_rmsnorm(hp, norm_ffn[i]), moe_w_router[j], moe_b_router[j], moe_w_gu[j], moe_w_down[j])
            hs = hs + _moe(_rmsnorm(hs, norm_ffn[i]), moe_w_router[j], moe_b_router[j], moe_w_gu[j], moe_w_down[j])
    y_prompt = _rmsnorm(hp, norm_final)
    y_sample = _rmsnorm(hs, norm_final)
    return (y_prompt, y_sample, jnp.stack(k_p), jnp.stack(v_p), jnp.stack(k_s), jnp.stack(v_s),
            jnp.stack(ssm_p), jnp.stack(conv_p), jnp.stack(ssm_s), jnp.stack(conv_s))
```

```python
import functools
import math

import jax
import jax.numpy as jnp
from jax import lax
from jax.experimental import pallas as pl
from jax.experimental.pallas import tpu as pltpu

F32 = jnp.float32
BF16 = jnp.bfloat16
HIGHEST = lax.Precision.HIGHEST

EPS = 1e-6
NEG_INF = -1e30
ROPE_THETA = 10000.0
PAGE_SIZE = 128
GDN_CHUNK = 64
CONV_W = 4
TOP_K = 2

LANES = 128
SUBLANES = 8
VMEM_LIMIT = 56 * 1024 * 1024

NT_DIMS = (((1,), (1,)), ((), ()))
TN_DIMS = (((0,), (0,)), ((), ()))


def _params(n_axes):
    return pltpu.CompilerParams(dimension_semantics=("arbitrary",) * n_axes,
                                vmem_limit_bytes=VMEM_LIMIT)


def _tile(n, pref):
    if n <= pref:
        return n
    t = pref
    while n % t:
        t //= 2
    return t


def _sigmoid(x):
    return 1.0 / (1.0 + jnp.exp(-x))


def _softplus(x):
    return jnp.maximum(x, 0.0) + jnp.log(1.0 + jnp.exp(-jnp.abs(x)))


def _rms_matmul_kernel(x_ref, g_ref, w_ref, o_ref, xn_ref, *, precision):
    @pl.when(pl.program_id(1) == 0)
    def _():
        x = x_ref[...]
        r = lax.rsqrt(jnp.mean(x * x, axis=-1, keepdims=True) + EPS)
        xn_ref[...] = (x * r * g_ref[...]).astype(xn_ref.dtype)

    o_ref[...] = jnp.dot(xn_ref[...], w_ref[...], preferred_element_type=F32,
                         precision=precision)


def rms_matmul(x, g, w, *, exact):
    n, d = x.shape
    n_out = w.shape[1]
    tm = _tile(n, 512)
    tn = _tile(n_out, 512)
    return pl.pallas_call(
        functools.partial(_rms_matmul_kernel, precision=HIGHEST if exact else None),
        grid=(n // tm, n_out // tn),
        in_specs=[pl.BlockSpec((tm, d), lambda i, j: (i, 0)),
                  pl.BlockSpec((1, d), lambda i, j: (0, 0)),
                  pl.BlockSpec((d, tn), lambda i, j: (0, j))],
        out_specs=pl.BlockSpec((tm, tn), lambda i, j: (i, j)),
        out_shape=jax.ShapeDtypeStruct((n, n_out), F32),
        scratch_shapes=[pltpu.VMEM((tm, d), F32 if exact else BF16)],
        compiler_params=_params(2),
        name="rms_matmul_exact" if exact else "rms_matmul",
    )(x, g.reshape(1, d), w)


def _matmul_res_kernel(a_ref, w_ref, r_ref, o_ref, *, precision):
    o_ref[...] = r_ref[...] + jnp.dot(a_ref[...], w_ref[...], preferred_element_type=F32,
                                      precision=precision)


def matmul_res(a, w, res, *, exact):
    n, k = a.shape
    n_out = w.shape[1]
    tm = _tile(n, 512)
    tn = _tile(n_out, 512)
    return pl.pallas_call(
        functools.partial(_matmul_res_kernel, precision=HIGHEST if exact else None),
        grid=(n // tm, n_out // tn),
        in_specs=[pl.BlockSpec((tm, k), lambda i, j: (i, 0)),
                  pl.BlockSpec((k, tn), lambda i, j: (0, j)),
                  pl.BlockSpec((tm, tn), lambda i, j: (i, j))],
        out_specs=pl.BlockSpec((tm, tn), lambda i, j: (i, j)),
        out_shape=jax.ShapeDtypeStruct((n, n_out), F32),
        compiler_params=_params(2),
        name="matmul_res_exact" if exact else "matmul_res",
    )(a, w, res)


def _rope_tables(pos, half):
    inv = ROPE_THETA ** (-jnp.arange(half, dtype=F32) / half)
    ang = pos.astype(F32)[:, None] * inv[None, :]
    cos, sin = jnp.cos(ang), jnp.sin(ang)
    cos4 = jnp.concatenate([cos, cos, cos, cos], axis=-1)
    sin4 = jnp.concatenate([-sin, sin, -sin, sin], axis=-1)
    return cos4, sin4


def _rope_kernel(q_ref, k_ref, v_ref, cos_ref, sin_ref, qo_ref, ko_ref, kb_ref, vb_ref, *,
                 n_heads, q_scale):
    cos = cos_ref[...]
    sin = sin_ref[...]
    quarter = LANES // 4
    lane = lax.broadcasted_iota(jnp.int32, cos.shape, 1)
    low = (lane % (2 * quarter)) < quarter
    for h in range(n_heads):
        sl = slice(h * LANES, (h + 1) * LANES)
        for src, scale, dsts in ((q_ref, q_scale, (qo_ref,)), (k_ref, 1.0, (ko_ref, kb_ref))):
            x = src[:, sl]
            partner = jnp.where(low, pltpu.roll(x, LANES - quarter, 1), pltpu.roll(x, quarter, 1))
            y = x * cos + partner * sin
            if scale != 1.0:
                y = y * scale
            for dst in dsts:
                dst[:, sl] = y.astype(dst.dtype)
    vb_ref[...] = v_ref[...].astype(vb_ref.dtype)


def rope_split(qkv, cos4, sin4, *, n_heads, rows_per_seq, q_dtype):
    n = qkv.shape[0]
    w = n_heads * LANES
    tm = _tile(rows_per_seq, 512)
    per_seq = rows_per_seq // tm
    q_scale = (LANES // 2) ** -0.5
    blk = lambda c: pl.BlockSpec((tm, w), lambda i: (i, c))
    tab = pl.BlockSpec((tm, LANES), lambda i: (i % per_seq, 0))
    return pl.pallas_call(
        functools.partial(_rope_kernel, n_heads=n_heads, q_scale=q_scale),
        grid=(n // tm,),
        in_specs=[blk(0), blk(1), blk(2), tab, tab],
        out_specs=[blk(0), blk(0), blk(0), blk(0)],
        out_shape=[jax.ShapeDtypeStruct((n, w), q_dtype),
                   jax.ShapeDtypeStruct((n, w), F32),
                   jax.ShapeDtypeStruct((n, w), BF16),
                   jax.ShapeDtypeStruct((n, w), BF16)],
        compiler_params=_params(1),
        name="rope_split",
    )(qkv, qkv, qkv, cos4, sin4)


def _diff_lambda_in_kernel(lam_ref, lam_init):
    lq = lam_ref[...]
    a = jnp.sum(lq[0:1] * lq[1:2], axis=-1, keepdims=True)
    b = jnp.sum(lq[2:3] * lq[3:4], axis=-1, keepdims=True)
    return jnp.exp(a) - jnp.exp(b) + lam_init


def _subln(o, subln, lam_init):
    r = lax.rsqrt(jnp.mean(o * o, axis=-1, keepdims=True) + EPS)
    return (o * r * subln) * (1.0 - lam_init)


def _flash_kernel(lam_ref, subln_ref, q_ref, k_ref, v_ref, o_ref,
                  qa_ref, qb_ref, m_ref, l_ref, acc_ref, *, tq, tk, lam_init):
    qi = pl.program_id(2)
    ki = pl.program_id(3)
    last = ((qi + 1) * tq - 1) // tk
    first_masked = (qi * tq) // tk

    @pl.when(ki == 0)
    def _():
        q = q_ref[...]
        lane = lax.broadcasted_iota(jnp.int32, q.shape, 1)
        zero = jnp.zeros_like(q)
        qa_ref[...] = jnp.where(lane < LANES // 2, q, zero)
        qb_ref[...] = jnp.where(lane >= LANES // 2, q, zero)
        m_ref[...] = jnp.full(m_ref.shape, NEG_INF, F32)
        l_ref[...] = jnp.zeros(l_ref.shape, F32)
        acc_ref[...] = jnp.zeros(acc_ref.shape, F32)

    def step(masked):
        k = k_ref[...]
        v = v_ref[...]
        if masked:
            row = qi * tq + lax.broadcasted_iota(jnp.int32, (tq, tk), 0)
            col = ki * tk + lax.broadcasted_iota(jnp.int32, (tq, tk), 1)
            keep = col <= row
        for c, qx_ref in enumerate((qa_ref, qb_ref)):
            s = lax.dot_general(qx_ref[...], k, NT_DIMS, preferred_element_type=F32)
            if masked:
                s = jnp.where(keep, s, NEG_INF)
            m_prev = m_ref[c]
            m_new = jnp.maximum(m_prev, jnp.max(s, axis=-1, keepdims=True))
            alpha = jnp.exp(m_prev - m_new)
            p = jnp.exp(s - m_new)
            l_ref[c] = alpha * l_ref[c] + jnp.sum(p, axis=-1, keepdims=True)
            acc_ref[c] = alpha * acc_ref[c] + jnp.dot(p.astype(v.dtype), v,
                                                      preferred_element_type=F32)
            m_ref[c] = m_new

    @pl.when(ki < first_masked)
    def _():
        step(False)

    @pl.when(jnp.logical_and(ki >= first_masked, ki <= last))
    def _():
        step(True)

    @pl.when(ki == last)
    def _():
        lam = _diff_lambda_in_kernel(lam_ref, lam_init)
        o = acc_ref[0] / l_ref[0] - lam * (acc_ref[1] / l_ref[1])
        o_ref[...] = _subln(o, subln_ref[...], lam_init).astype(o_ref.dtype)


def flash_diff_attention(q, k, v, lam_qk, subln, *, batch, seq, n_heads, lam_init):
    n = batch * seq
    tq = tk = _tile(seq, 512)
    nq, nk = seq // tq, seq // tk

    def kv_map(b, h, qi, ki):
        return (b * nk + jnp.minimum(ki, ((qi + 1) * tq - 1) // tk), h)

    return pl.pallas_call(
        functools.partial(_flash_kernel, tq=tq, tk=tk, lam_init=lam_init),
        grid=(batch, n_heads, nq, nk),
        in_specs=[pl.BlockSpec(lam_qk.shape, lambda b, h, qi, ki: (0, 0)),
                  pl.BlockSpec((1, LANES), lambda b, h, qi, ki: (0, 0)),
                  pl.BlockSpec((tq, LANES), lambda b, h, qi, ki: (b * nq + qi, h)),
                  pl.BlockSpec((tk, LANES), kv_map),
                  pl.BlockSpec((tk, LANES), kv_map)],
        out_specs=pl.BlockSpec((tq, LANES), lambda b, h, qi, ki: (b * nq + qi, h)),
        out_shape=jax.ShapeDtypeStruct((n, n_heads * LANES), BF16),
        scratch_shapes=[pltpu.VMEM((tq, LANES), BF16), pltpu.VMEM((tq, LANES), BF16),
                        pltpu.VMEM((2, tq, 1), F32), pltpu.VMEM((2, tq, 1), F32),
                        pltpu.VMEM((2, tq, LANES), F32)],
        compiler_params=_params(4),
        name="flash_diff_attention",
    )(lam_qk, subln.reshape(1, LANES), q, k, v)


def _decode_kernel(pt_ref, lam_ref, subln_ref, q_ref, kn_ref, vn_ref, *rest,
                   pages_per_step, n_heads, lam_init):
    g = pages_per_step
    k_refs = rest[:g]
    v_refs = rest[g:2 * g]
    o_ref, m_ref, l_ref, acc_ref = rest[2 * g:]
    step = pl.program_id(1)
    rows = 2 * n_heads
    cols = PAGE_SIZE * n_heads

    @pl.when(step == 0)
    def _():
        m_ref[...] = jnp.full(m_ref.shape, NEG_INF, F32)
        l_ref[...] = jnp.zeros(l_ref.shape, F32)
        acc_ref[...] = jnp.zeros(acc_ref.shape, F32)

    q = q_ref[0]
    row_head = lax.broadcasted_iota(jnp.int32, (rows, cols), 0) % n_heads
    col_head = lax.broadcasted_iota(jnp.int32, (rows, cols), 1) % n_heads
    same_head = row_head == col_head
    for j in range(g):
        kp = k_refs[j][0, 0].reshape(cols, LANES)
        vp = v_refs[j][0, 0].reshape(cols, LANES)
        s = lax.dot_general(q, kp, NT_DIMS, preferred_element_type=F32)
        s = jnp.where(same_head, s, NEG_INF)
        m_prev = m_ref[...]
        m_new = jnp.maximum(m_prev, jnp.max(s, axis=-1, keepdims=True))
        alpha = jnp.exp(m_prev - m_new)
        p = jnp.exp(s - m_new)
        l_ref[...] = alpha * l_ref[...] + jnp.sum(p, axis=-1, keepdims=True)
        acc_ref[...] = alpha * acc_ref[...] + jnp.dot(p, vp, preferred_element_type=F32)
        m_ref[...] = m_new

    @pl.when(step == pl.num_programs(1) - 1)
    def _():
        s_new = jnp.sum(q * kn_ref[0], axis=-1, keepdims=True)
        m_prev = m_ref[...]
        m_new = jnp.maximum(m_prev, s_new)
        alpha = jnp.exp(m_prev - m_new)
        p_new = jnp.exp(s_new - m_new)
        l = alpha * l_ref[...] + p_new
        acc = alpha * acc_ref[...] + p_new * vn_ref[0]
        o2 = acc / l
        lam = _diff_lambda_in_kernel(lam_ref, lam_init)
        o = o2[:n_heads] - lam * o2[n_heads:]
        o_ref[0] = _subln(o, subln_ref[...], lam_init)


def decode_diff_attention(q2, kn2, vn2, cache_k, cache_v, page_table, lam_qk, subln, *,
                          layer, lam_init):
    bd, rows, _ = q2.shape
    n_heads = rows // 2
    n_pages = page_table.shape[1]
    g = _tile(n_pages, 8)
    steps = n_pages // g

    def page_spec(j):
        return pl.BlockSpec((1, 1, PAGE_SIZE, n_heads, LANES),
                            lambda b, s, pt: (layer, pt[b * n_pages + s * g + j], 0, 0, 0))

    small = pl.BlockSpec((1, rows, LANES), lambda b, s, pt: (b, 0, 0))
    grid_spec = pltpu.PrefetchScalarGridSpec(
        num_scalar_prefetch=1,
        grid=(bd, steps),
        in_specs=[pl.BlockSpec(lam_qk.shape, lambda b, s, pt: (0, 0)),
                  pl.BlockSpec((1, LANES), lambda b, s, pt: (0, 0)),
                  small, small, small]
                 + [page_spec(j) for j in range(g)] + [page_spec(j) for j in range(g)],
        out_specs=pl.BlockSpec((1, n_heads, LANES), lambda b, s, pt: (b, 0, 0)),
        scratch_shapes=[pltpu.VMEM((rows, 1), F32), pltpu.VMEM((rows, 1), F32),
                        pltpu.VMEM((rows, LANES), F32)],
    )
    return pl.pallas_call(
        functools.partial(_decode_kernel, pages_per_step=g, n_heads=n_heads, lam_init=lam_init),
        grid_spec=grid_spec,
        out_shape=jax.ShapeDtypeStruct((bd, n_heads, LANES), F32),
        compiler_params=_params(2),
        name="decode_diff_attention",
    )(page_table.reshape(-1), lam_qk, subln.reshape(1, LANES), q2, kn2, vn2,
      *([cache_k] * g), *([cache_v] * g))


def _ffn_kernel(te_ref, nu_ref, x_ref, g_ref, wg_ref, wu_ref, wd_ref, o_ref, xn_ref, acc_ref, *,
                precision, residual):
    i = pl.program_id(0)
    j = pl.program_id(1)
    used = i < nu_ref[0]

    @pl.when(jnp.logical_and(used, j == 0))
    def _():
        x = x_ref[...]
        r = lax.rsqrt(jnp.mean(x * x, axis=-1, keepdims=True) + EPS)
        xn_ref[...] = (x * r * g_ref[...]).astype(xn_ref.dtype)
        acc_ref[...] = jnp.zeros(acc_ref.shape, F32)

    @pl.when(used)
    def _():
        xn = xn_ref[...]
        gate = jnp.dot(xn, wg_ref[0], preferred_element_type=F32, precision=precision)
        up = jnp.dot(xn, wu_ref[0], preferred_element_type=F32, precision=precision)
        hidden = (gate * _sigmoid(gate)) * up
        acc_ref[...] += jnp.dot(hidden.astype(xn.dtype), wd_ref[0], preferred_element_type=F32,
                                precision=precision)

    last = j == pl.num_programs(1) - 1

    @pl.when(jnp.logical_and(used, last))
    def _():
        o_ref[...] = (x_ref[...] + acc_ref[...]) if residual else acc_ref[...]

    @pl.when(jnp.logical_and(jnp.logical_not(used), last))
    def _():
        o_ref[...] = x_ref[...] if residual else jnp.zeros(o_ref.shape, F32)


def swiglu_ffn(x, g, w_gu, w_down, tile_expert, n_used, *, tm, exact, residual):
    p, d = x.shape
    f = w_down.shape[1]
    tf = _tile(f, 512)
    nf = f // tf
    grid_spec = pltpu.PrefetchScalarGridSpec(
        num_scalar_prefetch=2,
        grid=(p // tm, nf),
        in_specs=[pl.BlockSpec((tm, d), lambda i, j, te, nu: (i, 0)),
                  pl.BlockSpec((1, d), lambda i, j, te, nu: (0, 0)),
                  pl.BlockSpec((1, d, tf), lambda i, j, te, nu: (te[i], 0, j)),
                  pl.BlockSpec((1, d, tf), lambda i, j, te, nu: (te[i], 0, nf + j)),
                  pl.BlockSpec((1, tf, d), lambda i, j, te, nu: (te[i], j, 0))],
        out_specs=pl.BlockSpec((tm, d), lambda i, j, te, nu: (i, 0)),
        scratch_shapes=[pltpu.VMEM((tm, d), F32 if exact else BF16),
                        pltpu.VMEM((tm, d), F32)],
    )
    return pl.pallas_call(
        functools.partial(_ffn_kernel, precision=HIGHEST if exact else None, residual=residual),
        grid_spec=grid_spec,
        out_shape=jax.ShapeDtypeStruct((p, d), F32),
        compiler_params=_params(2),
        name="swiglu_ffn_exact" if exact else "swiglu_ffn",
    )(tile_expert, n_used, x, g.reshape(1, d), w_gu, w_gu, w_down)


def _gdn_conv_kernel(x_ref, w_ref, o_ref, buf_ref, *, tm, n_qk_heads, n_q_heads, q_scale):
    c = pl.program_id(1)
    i = pl.program_id(2)

    @pl.when(i == 0)
    def _():
        buf_ref[0:SUBLANES, :] = jnp.zeros((SUBLANES, LANES), F32)

    x = x_ref[...]
    buf_ref[SUBLANES:SUBLANES + tm, :] = x
    w = w_ref[...]
    y = buf_ref[SUBLANES - 3:SUBLANES - 3 + tm, :] * w[0:1]
    y = y + buf_ref[SUBLANES - 2:SUBLANES - 2 + tm, :] * w[1:2]
    y = y + buf_ref[SUBLANES - 1:SUBLANES - 1 + tm, :] * w[2:3]
    y = y + x * w[3:4]
    y = y * _sigmoid(y)
    r = lax.rsqrt(jnp.sum(y * y, axis=-1, keepdims=True) + EPS)
    scale = jnp.where(c < n_qk_heads, r, 1.0) * jnp.where(c < n_q_heads, q_scale, 1.0)
    o_ref[...] = y * scale
    buf_ref[0:SUBLANES, :] = x[tm - SUBLANES:tm, :]


def gdn_conv_silu(proj, w_conv, *, batch, seq, n_heads):
    n = batch * seq
    c_blocks = w_conv.shape[1] // LANES
    tm = _tile(seq, 1024)
    nt = seq // tm
    return pl.pallas_call(
        functools.partial(_gdn_conv_kernel, tm=tm, n_qk_heads=2 * n_heads, n_q_heads=n_heads,
                          q_scale=LANES ** -0.5),
        grid=(batch, c_blocks, nt),
        in_specs=[pl.BlockSpec((tm, LANES), lambda b, c, i: (b * nt + i, c)),
                  pl.BlockSpec((CONV_W, LANES), lambda b, c, i: (0, c))],
        out_specs=pl.BlockSpec((tm, LANES), lambda b, c, i: (b * nt + i, c)),
        out_shape=jax.ShapeDtypeStruct((n, c_blocks * LANES), F32),
        scratch_shapes=[pltpu.VMEM((tm + SUBLANES, LANES), F32)],
        compiler_params=_params(3),
        name="gdn_conv",
    )(proj, w_conv)


def _gdn_gates_kernel(t_ref, alog_ref, dtb_ref, o_ref, *, tm, n_heads):
    t = t_ref[...]
    g = -jnp.exp(alog_ref[...]) * _softplus(t + dtb_ref[...])
    beta = _sigmoid(t)
    row = lax.broadcasted_iota(jnp.int32, (tm, tm), 0)
    col = lax.broadcasted_iota(jnp.int32, (tm, tm), 1)
    tri = jnp.where(jnp.logical_and(row // GDN_CHUNK == col // GDN_CHUNK, col <= row), 1.0, 0.0)
    gsum = jnp.dot(tri, g, preferred_element_type=F32, precision=HIGHEST)
    lane = lax.broadcasted_iota(jnp.int32, t.shape, 1)
    o_ref[...] = jnp.where(lane < n_heads, gsum, beta)


def gdn_gates(proj, a_log, dt_bias, *, n_heads, col_block):
    n = proj.shape[0]
    tm = _tile(n, 256)
    pad = lambda v: jnp.zeros((1, LANES), F32).at[0, :n_heads].set(v.astype(F32))
    return pl.pallas_call(
        functools.partial(_gdn_gates_kernel, tm=tm, n_heads=n_heads),
        grid=(n // tm,),
        in_specs=[pl.BlockSpec((tm, LANES), lambda i: (i, col_block)),
                  pl.BlockSpec((1, LANES), lambda i: (0, 0)),
                  pl.BlockSpec((1, LANES), lambda i: (0, 0))],
        out_specs=pl.BlockSpec((tm, LANES), lambda i: (i, 0)),
        out_shape=jax.ShapeDtypeStruct((n, LANES), F32),
        compiler_params=_params(1),
        name="gdn_gates",
    )(proj, pad(a_log), pad(dt_bias))


def _gdn_chunk_kernel(q_ref, k_ref, v_ref, z_ref, gb_ref, gr_ref, onorm_ref, o_ref, s_out_ref,
                      s_ref, *, chunks, n_heads):
    h = pl.program_id(1)
    i = pl.program_id(2)
    cs = GDN_CHUNK

    @pl.when(i == 0)
    def _():
        s_ref[...] = jnp.zeros(s_ref.shape, F32)

    row = lax.broadcasted_iota(jnp.int32, (cs, cs), 0)
    col = lax.broadcasted_iota(jnp.int32, (cs, cs), 1)
    lower = row >= col
    strict = row > col
    eye = jnp.where(row == col, 1.0, 0.0)
    lane = lax.broadcasted_iota(jnp.int32, (cs, LANES), 1)
    onorm = onorm_ref[...]

    for c in range(chunks):
        sl = slice(c * cs, (c + 1) * cs)
        q, k, v, z = q_ref[sl, :], k_ref[sl, :], v_ref[sl, :], z_ref[sl, :]
        gb = gb_ref[sl, :]
        g_col = jnp.sum(jnp.where(lane == h, gb, 0.0), axis=-1, keepdims=True)
        b_col = jnp.sum(jnp.where(lane == h + n_heads, gb, 0.0), axis=-1, keepdims=True)
        g_row = gr_ref[c, pl.ds(h, 1), :]
        g_last = g_row[:, cs - 1:cs]
        decay = jnp.exp(jnp.where(lower, g_col - g_row, NEG_INF))
        kb = k * b_col
        kb16, k16, q16 = kb.astype(BF16), k.astype(BF16), q.astype(BF16)
        a = jnp.where(strict, lax.dot_general(kb16, k16, NT_DIMS, preferred_element_type=F32)
                      * decay, 0.0)
        x = -a
        t = eye + x
        x = jnp.dot(x, x, preferred_element_type=F32, precision=HIGHEST)
        for _ in range(int(math.log2(cs)) - 1):
            tx = jnp.dot(jnp.concatenate([t, x], axis=0), x, preferred_element_type=F32,
                         precision=HIGHEST)
            t = t + tx[:cs]
            x = tx[cs:]
        e_col = jnp.exp(g_col)
        rhs = jnp.concatenate([v * b_col, kb * e_col], axis=1)
        sol = jnp.dot(t, rhs, preferred_element_type=F32, precision=HIGHEST)
        u0, w = sol[:, :LANES], sol[:, LANES:]
        qk = lax.dot_general(q16, k16, NT_DIMS, preferred_element_type=F32) * decay
        q_dec = q * e_col
        k_dec = k * jnp.exp(g_last - g_col)
        s = s_ref[...]
        s16 = s.astype(BF16)
        u = u0 - jnp.dot(w.astype(BF16), s16, preferred_element_type=F32)
        u16 = u.astype(BF16)
        o = (jnp.dot(q_dec.astype(BF16), s16, preferred_element_type=F32)
             + jnp.dot(qk.astype(BF16), u16, preferred_element_type=F32))
        s_ref[...] = s * jnp.exp(g_last) + lax.dot_general(k_dec, u, TN_DIMS,
                                                           preferred_element_type=F32)
        r = lax.rsqrt(jnp.mean(o * o, axis=-1, keepdims=True) + EPS)
        o_ref[sl, :] = ((o * r * onorm) * (z * _sigmoid(z))).astype(o_ref.dtype)

    @pl.when(i == pl.num_programs(2) - 1)
    def _():
        s_out_ref[0, 0] = s_ref[...]


def gdn_chunked(qkv, proj, gb, g_rows, onorm, *, batch, seq, n_heads, z_col_block):
    n = batch * seq
    tt = _tile(seq, 256)
    chunks = tt // GDN_CHUNK
    nt = seq // tt
    blk = lambda off: pl.BlockSpec((tt, LANES), lambda b, h, i: (b * nt + i, off + h))
    return pl.pallas_call(
        functools.partial(_gdn_chunk_kernel, chunks=chunks, n_heads=n_heads),
        grid=(batch, n_heads, nt),
        in_specs=[blk(0), blk(n_heads), blk(2 * n_heads),
                  pl.BlockSpec((tt, LANES), lambda b, h, i: (b * nt + i, z_col_block + h)),
                  pl.BlockSpec((tt, LANES), lambda b, h, i: (b * nt + i, 0)),
                  pl.BlockSpec((chunks, n_heads, GDN_CHUNK), lambda b, h, i: (b * nt + i, 0, 0)),
                  pl.BlockSpec((1, LANES), lambda b, h, i: (0, 0))],
        out_specs=[pl.BlockSpec((tt, LANES), lambda b, h, i: (b * nt + i, h)),
                   pl.BlockSpec((1, 1, LANES, LANES), lambda b, h, i: (b, h, 0, 0))],
        out_shape=[jax.ShapeDtypeStruct((n, n_heads * LANES), BF16),
                   jax.ShapeDtypeStruct((batch, n_heads, LANES, LANES), F32)],
        scratch_shapes=[pltpu.VMEM((LANES, LANES), F32)],
        compiler_params=_params(3),
        name="gdn_chunked",
    )(qkv, qkv, qkv, proj, gb, g_rows, onorm.reshape(1, LANES))


def _gdn_step_kernel(proj_ref, conv_ref, w_ref, alog_ref, dtb_ref, onorm_ref, s_ref,
                     o_ref, conv_out_ref, s_out_ref, *, n_heads):
    width = n_heads * LANES
    x = proj_ref[0]
    qkv = x[:, :3 * width]
    st = conv_ref[0]
    w = w_ref[...]
    y = st[0:1] * w[0:1] + st[1:2] * w[1:2] + st[2:3] * w[2:3] + qkv * w[3:4]
    y = y * _sigmoid(y)
    conv_out_ref[0, 0:CONV_W - 2, :] = st[1:CONV_W - 1]
    conv_out_ref[0, CONV_W - 2:CONV_W - 1, :] = qkv

    tail = x[:, 4 * width:]
    g_vec = -jnp.exp(alog_ref[...]) * _softplus(tail + dtb_ref[...])
    beta_vec = _sigmoid(tail)
    lane = lax.broadcasted_iota(jnp.int32, (1, LANES), 1)
    row = lax.broadcasted_iota(jnp.int32, (LANES, LANES), 0)
    col = lax.broadcasted_iota(jnp.int32, (LANES, LANES), 1)
    eye = jnp.where(row == col, 1.0, 0.0)
    onorm = onorm_ref[...]

    def column(v_row):
        return jnp.sum(eye * v_row, axis=-1, keepdims=True)

    for h in range(n_heads):
        q = y[:, h * LANES:(h + 1) * LANES]
        k = y[:, width + h * LANES: width + (h + 1) * LANES]
        v = y[:, 2 * width + h * LANES: 2 * width + (h + 1) * LANES]
        z = x[:, 3 * width + h * LANES: 3 * width + (h + 1) * LANES]
        q = q * lax.rsqrt(jnp.sum(q * q, axis=-1, keepdims=True) + EPS) * (LANES ** -0.5)
        k = k * lax.rsqrt(jnp.sum(k * k, axis=-1, keepdims=True) + EPS)
        g = jnp.sum(jnp.where(lane == h, g_vec, 0.0), axis=-1, keepdims=True)
        beta = jnp.sum(jnp.where(lane == h + n_heads, beta_vec, 0.0), axis=-1, keepdims=True)
        k_col, q_col = column(k), column(q)
        s = s_ref[0, h] * jnp.exp(g)
        u = beta * (v - jnp.sum(k_col * s, axis=0, keepdims=True))
        s = s + k_col * u
        o = jnp.sum(q_col * s, axis=0, keepdims=True)
        s_out_ref[0, h] = s
        r = lax.rsqrt(jnp.mean(o * o, axis=-1, keepdims=True) + EPS)
        o_ref[0, :, h * LANES:(h + 1) * LANES] = (o * r * onorm) * (z * _sigmoid(z))


def gdn_step(proj, conv_state, w_conv, a_log, dt_bias, onorm, s0, *, n_heads):
    bd, total = proj.shape
    width = n_heads * LANES
    pad = lambda v: jnp.zeros((1, LANES), F32).at[0, :n_heads].set(v.astype(F32))
    vec = pl.BlockSpec((1, LANES), lambda b: (0, 0))
    return pl.pallas_call(
        functools.partial(_gdn_step_kernel, n_heads=n_heads),
        grid=(bd,),
        in_specs=[pl.BlockSpec((1, 1, total), lambda b: (b, 0, 0)),
                  pl.BlockSpec((1, CONV_W - 1, 3 * width), lambda b: (b, 0, 0)),
                  pl.BlockSpec((CONV_W, 3 * width), lambda b: (0, 0)),
                  vec, vec, vec,
                  pl.BlockSpec((1, n_heads, LANES, LANES), lambda b: (b, 0, 0, 0))],
        out_specs=[pl.BlockSpec((1, 1, width), lambda b: (b, 0, 0)),
                   pl.BlockSpec((1, CONV_W - 1, 3 * width), lambda b: (b, 0, 0)),
                   pl.BlockSpec((1, n_heads, LANES, LANES), lambda b: (b, 0, 0, 0))],
        out_shape=[jax.ShapeDtypeStruct((bd, 1, width), F32),
                   jax.ShapeDtypeStruct((bd, CONV_W - 1, 3 * width), F32),
                   jax.ShapeDtypeStruct((bd, n_heads, LANES, LANES), F32)],
        compiler_params=_params(1),
        name="gdn_step",
    )(proj.reshape(bd, 1, total), conv_state, w_conv, pad(a_log), pad(dt_bias),
      onorm.reshape(1, LANES), s0)


def _router_kernel(x_ref, g_ref, w_ref, b_ref, o_ref, *, precision, cast):
    x = x_ref[...]
    r = lax.rsqrt(jnp.mean(x * x, axis=-1, keepdims=True) + EPS)
    xn = (x * r * g_ref[...]).astype(cast)
    logits = jnp.dot(xn, w_ref[...], preferred_element_type=F32, precision=precision) + b_ref[...]
    lane = lax.broadcasted_iota(jnp.int32, logits.shape, 1)
    v1 = jnp.max(logits, axis=-1, keepdims=True)
    i1 = jnp.min(jnp.where(logits == v1, lane, LANES), axis=-1, keepdims=True)
    rest = jnp.where(lane == i1, NEG_INF * 2, logits)
    v2 = jnp.max(rest, axis=-1, keepdims=True)
    i2 = jnp.min(jnp.where(rest == v2, lane, LANES), axis=-1, keepdims=True)
    e2 = jnp.exp(v2 - v1)
    g1 = 1.0 / (1.0 + e2)
    g2 = e2 / (1.0 + e2)
    out = jnp.where(lane == 0, i1.astype(F32),
                    jnp.where(lane == 1, i2.astype(F32),
                              jnp.where(lane == 2, g1, jnp.where(lane == 3, g2, 0.0))))
    o_ref[...] = out


def router(x, g, w_router, b_router, *, exact):
    n, d = x.shape
    n_exp = w_router.shape[1]
    tm = _tile(n, 512)
    wdt = F32 if exact else BF16
    w_pad = jnp.zeros((d, LANES), wdt).at[:, :n_exp].set(w_router.astype(wdt))
    b_pad = jnp.full((1, LANES), NEG_INF, F32).at[0, :n_exp].set(b_router.astype(F32))
    return pl.pallas_call(
        functools.partial(_router_kernel, precision=HIGHEST if exact else None, cast=wdt),
        grid=(n // tm,),
        in_specs=[pl.BlockSpec((tm, d), lambda i: (i, 0)),
                  pl.BlockSpec((1, d), lambda i: (0, 0)),
                  pl.BlockSpec((d, LANES), lambda i: (0, 0)),
                  pl.BlockSpec((1, LANES), lambda i: (0, 0))],
        out_specs=pl.BlockSpec((tm, LANES), lambda i: (i, 0)),
        out_shape=jax.ShapeDtypeStruct((n, LANES), F32),
        compiler_params=_params(1),
        name="router_exact" if exact else "router",
    )(x, g.reshape(1, d), w_pad, b_pad)


GATHER_BATCH = 512


def _row_copy(src, src_row, dst, dst_row, sem):
    return pltpu.make_async_copy(src.at[pl.ds(src_row, 1)], dst.at[pl.ds(dst_row, 1)], sem)


def _dispatch_kernel(tok_ref, sslot_ref, xp_ref, xs_ref, o_ref, sem, *, n_slots, n_sample_slots,
                     top_k):
    def batch(bi, carry):
        base = bi * GATHER_BATCH

        def start(s, c):
            _row_copy(xp_ref, tok_ref[base + s], o_ref, base + s, sem).start()
            return c

        def wait(s, c):
            _row_copy(xp_ref, 0, o_ref, base + s, sem).wait()
            return c

        lax.fori_loop(0, GATHER_BATCH, start, 0)
        lax.fori_loop(0, GATHER_BATCH, wait, 0)
        return carry

    lax.fori_loop(0, n_slots // GATHER_BATCH, batch, 0)

    def start_s(j, c):
        _row_copy(xs_ref, j // top_k, o_ref, sslot_ref[j], sem).start()
        return c

    def wait_s(j, c):
        _row_copy(xs_ref, 0, o_ref, sslot_ref[j], sem).wait()
        return c

    lax.fori_loop(0, n_sample_slots, start_s, 0)
    lax.fori_loop(0, n_sample_slots, wait_s, 0)


def dispatch_rows(tok_of_slot, sample_slots, x_prompt_rows, x_sample_rows):
    n_slots = tok_of_slot.shape[0]
    d = x_prompt_rows.shape[1]
    grid_spec = pltpu.PrefetchScalarGridSpec(
        num_scalar_prefetch=2,
        grid=(1,),
        in_specs=[pl.BlockSpec(memory_space=pl.ANY), pl.BlockSpec(memory_space=pl.ANY)],
        out_specs=pl.BlockSpec(memory_space=pl.ANY),
        scratch_shapes=[pltpu.SemaphoreType.DMA(())],
    )
    return pl.pallas_call(
        functools.partial(_dispatch_kernel, n_slots=n_slots,
                          n_sample_slots=sample_slots.shape[0], top_k=TOP_K),
        grid_spec=grid_spec,
        out_shape=jax.ShapeDtypeStruct((n_slots, d), F32),
        compiler_params=_params(1),
        name="moe_dispatch",
    )(tok_of_slot, sample_slots, x_prompt_rows, x_sample_rows)


def _combine_kernel(slot_ref, h_ref, route_ref, g_ref, y_hbm, o_ref, buf_ref, sem, *, tm, top_k):
    base = pl.program_id(0) * tm * top_k

    def start(r, c):
        for kk in range(top_k):
            _row_copy(y_hbm, slot_ref[base + r * top_k + kk], buf_ref.at[kk], r, sem).start()
        return c

    def wait(r, c):
        for kk in range(top_k):
            _row_copy(y_hbm, 0, buf_ref.at[kk], r, sem).wait()
        return c

    lax.fori_loop(0, tm, start, 0)
    lax.fori_loop(0, tm, wait, 0)

    route = route_ref[...]
    lane = lax.broadcasted_iota(jnp.int32, route.shape, 1)
    y = jnp.zeros(h_ref.shape, F32)
    for kk in range(top_k):
        gate = jnp.sum(jnp.where(lane == top_k + kk, route, 0.0), axis=-1, keepdims=True)
        y = y + gate * buf_ref[kk]
    hn = h_ref[...] + y
    r = lax.rsqrt(jnp.mean(hn * hn, axis=-1, keepdims=True) + EPS)
    o_ref[...] = hn * r * g_ref[...]


def combine_rows(h, route, slots, y_sorted, g):
    n, d = h.shape
    tm = _tile(n, 256)
    grid_spec = pltpu.PrefetchScalarGridSpec(
        num_scalar_prefetch=1,
        grid=(n // tm,),
        in_specs=[pl.BlockSpec((tm, d), lambda i, s: (i, 0)),
                  pl.BlockSpec((tm, LANES), lambda i, s: (i, 0)),
                  pl.BlockSpec((1, d), lambda i, s: (0, 0)),
                  pl.BlockSpec(memory_space=pl.ANY)],
        out_specs=pl.BlockSpec((tm, d), lambda i, s: (i, 0)),
        scratch_shapes=[pltpu.VMEM((TOP_K, tm, d), F32), pltpu.SemaphoreType.DMA(())],
    )
    return pl.pallas_call(
        functools.partial(_combine_kernel, tm=tm, top_k=TOP_K),
        grid_spec=grid_spec,
        out_shape=jax.ShapeDtypeStruct((n, d), F32),
        compiler_params=_params(1),
        name="moe_combine",
    )(slots.reshape(-1), h, route, g.reshape(1, d), y_sorted)


def _plan_slots(expert_ids, n_experts, tm, n_prompt):
    t, k = expert_ids.shape
    flat = expert_ids.reshape(-1)
    n_assign = t * k
    max_tiles = -(-(n_assign + n_experts * (tm - 1)) // tm)
    n_slots = -(-max_tiles * tm // GATHER_BATCH) * GATHER_BATCH
    n_slots = -(-n_slots // tm) * tm
    n_tiles = n_slots // tm
    counts = jnp.zeros((n_experts,), jnp.int32).at[flat].add(1)
    tiles_per = (counts + tm - 1) // tm
    tile_end = jnp.cumsum(tiles_per)
    start = (tile_end - tiles_per) * tm
    order = jnp.argsort(flat, stable=True).astype(jnp.int32)
    sorted_e = flat[order]
    first = jnp.cumsum(counts) - counts
    slot_sorted = start[sorted_e] + (jnp.arange(n_assign, dtype=jnp.int32) - first[sorted_e])
    slot_of = jnp.zeros((n_assign,), jnp.int32).at[order].set(slot_sorted)
    row_of = order // k
    src = jnp.where(row_of < n_prompt, row_of, 0)
    tok_of_slot = jnp.zeros((n_slots,), jnp.int32).at[slot_sorted].set(src)
    tile_expert = jnp.minimum(
        jnp.searchsorted(tile_end, jnp.arange(n_tiles, dtype=jnp.int32), side="right"),
        n_experts - 1).astype(jnp.int32)
    n_used = tile_end[-1:].astype(jnp.int32)
    return tok_of_slot, slot_of.reshape(t, k), tile_expert, n_used


def kernel(x_prompt, x_sample, cache_k, cache_v, page_table, state_ssm, state_conv, norm_mix, norm_ffn, norm_final, attn_w_in, attn_lambda, attn_subln, attn_w_out, gdn_w_in, gdn_conv, gdn_a_log, gdn_dt_bias, gdn_onorm, gdn_w_out, ffn_w_gu, ffn_w_down, moe_w_router, moe_b_router, moe_w_gu, moe_w_down):
    batch, seq, d = x_prompt.shape
    bd, dec_t, _ = x_sample.shape
    assert dec_t == 1, "the sample group carries one new token per sequence"
    n = batch * seq
    heads_a = cache_k.shape[3]
    heads_b = state_ssm.shape[2]
    assert cache_k.shape[4] == LANES and cache_v.shape[4] == LANES
    assert state_ssm.shape[3] == LANES and state_ssm.shape[4] == LANES
    assert cache_k.shape[2] == PAGE_SIZE
    past_len = page_table.shape[1] * PAGE_SIZE
    wa = heads_a * LANES
    wb = heads_b * LANES
    n_experts = moe_w_router.shape[2]
    depth = norm_mix.shape[0]
    assert depth == 2, "layer 0 is differential attention, layer 1 is gated DeltaNet"

    hp = x_prompt.reshape(n, d)
    hs = x_sample.reshape(bd, d)

    lam_init = 0.8 - 0.6 * math.exp(-0.3 * 0)
    lam_qk = attn_lambda[0].astype(F32)
    w_in = attn_w_in[0]
    qkv_p = rms_matmul(hp, norm_mix[0], w_in.astype(BF16), exact=False)
    qkv_s = rms_matmul(hs, norm_mix[0], w_in, exact=True)
    half = LANES // 4
    cos_p, sin_p = _rope_tables(jnp.arange(seq, dtype=jnp.int32), half)
    cos_s, sin_s = _rope_tables(jnp.full((bd,), past_len, jnp.int32), half)
    q_p, k_p, kb_p, vb_p = rope_split(qkv_p, cos_p, sin_p, n_heads=heads_a, rows_per_seq=seq,
                                      q_dtype=BF16)
    q_s, k_s, _, _ = rope_split(qkv_s, cos_s, sin_s, n_heads=heads_a, rows_per_seq=bd,
                                q_dtype=F32)
    v_p = qkv_p[:, 2 * wa:]
    v_s = qkv_s[:, 2 * wa:]

    on_p = flash_diff_attention(q_p, kb_p, vb_p, lam_qk, attn_subln[0], batch=batch, seq=seq,
                                n_heads=heads_a, lam_init=lam_init)

    lane = jnp.arange(LANES)
    q3 = q_s.reshape(bd, heads_a, LANES)
    q2 = jnp.concatenate([jnp.where(lane < LANES // 2, q3, 0.0),
                          jnp.where(lane >= LANES // 2, q3, 0.0)], axis=1)
    kn2 = jnp.tile(k_s.reshape(bd, heads_a, LANES), (1, 2, 1))
    vn2 = jnp.tile(v_s.reshape(bd, heads_a, LANES), (1, 2, 1))
    on_s = decode_diff_attention(q2, kn2, vn2, cache_k, cache_v, page_table, lam_qk,
                                 attn_subln[0], layer=0, lam_init=lam_init)

    hp = matmul_res(on_p, attn_w_out[0].astype(BF16), hp, exact=False)
    hs = matmul_res(on_s.reshape(bd, wa), attn_w_out[0], hs, exact=True)

    one_tile = jnp.zeros((1,), jnp.int32)
    tm_p = _tile(n, 512)
    hp = swiglu_ffn(hp, norm_ffn[0], ffn_w_gu.astype(BF16), ffn_w_down.astype(BF16),
                    jnp.zeros((n // tm_p,), jnp.int32), jnp.full((1,), n // tm_p, jnp.int32),
                    tm=tm_p, exact=False, residual=True)
    hs = swiglu_ffn(hs, norm_ffn[0], ffn_w_gu, ffn_w_down, one_tile, jnp.ones((1,), jnp.int32),
                    tm=bd, exact=True, residual=True)

    c_conv = 3 * wb
    n_in = gdn_w_in.shape[2]
    tail = n_in - (c_conv + wb)
    assert tail == 2 * heads_b
    w_in_g = jnp.pad(gdn_w_in[0], ((0, 0), (0, LANES - tail)))
    proj_p = rms_matmul(hp, norm_mix[1], w_in_g.astype(BF16), exact=False)
    proj_s = rms_matmul(hs, norm_mix[1], w_in_g, exact=True)

    qkv_c = gdn_conv_silu(proj_p, gdn_conv[0], batch=batch, seq=seq, n_heads=heads_b)
    gb = gdn_gates(proj_p, gdn_a_log[0], gdn_dt_bias[0], n_heads=heads_b,
                   col_block=(c_conv + wb) // LANES)
    g_rows = gb[:, :heads_b].reshape(n // GDN_CHUNK, GDN_CHUNK, heads_b).transpose(0, 2, 1)
    gn_p, ssm_p = gdn_chunked(qkv_c, proj_p, gb, g_rows, gdn_onorm[0], batch=batch, seq=seq,
                              n_heads=heads_b, z_col_block=c_conv // LANES)
    conv_p = proj_p.reshape(batch, seq, -1)[:, seq - (CONV_W - 1):, :c_conv]
    hp = matmul_res(gn_p, gdn_w_out[0].astype(BF16), hp, exact=False)

    gn_s, conv_s, ssm_s = gdn_step(proj_s, state_conv[0], gdn_conv[0], gdn_a_log[0],
                                   gdn_dt_bias[0], gdn_onorm[0], state_ssm[0], n_heads=heads_b)
    hs = matmul_res(gn_s.reshape(bd, wb), gdn_w_out[0], hs, exact=True)

    route_p = router(hp, norm_ffn[1], moe_w_router[0], moe_b_router[0], exact=False)
    route_s = router(hs, norm_ffn[1], moe_w_router[0], moe_b_router[0], exact=True)
    ids = jnp.concatenate([route_p[:, :TOP_K], route_s[:, :TOP_K]], axis=0).astype(jnp.int32)
    tm_e = 512
    tok_of_slot, slot_of, tile_expert, n_used = _plan_slots(ids, n_experts, tm_e, n)
    x_sorted = dispatch_rows(tok_of_slot, slot_of[n:].reshape(-1), hp, hs)
    y_sorted = swiglu_ffn(x_sorted, norm_ffn[1], moe_w_gu[0].astype(BF16),
                          moe_w_down[0].astype(BF16), tile_expert, n_used,
                          tm=tm_e, exact=False, residual=False)
    y_p = combine_rows(hp, route_p, slot_of[:n], y_sorted, norm_final)
    y_s = combine_rows(hs, route_s, slot_of[n:], y_sorted, norm_final)

    return (y_p.reshape(batch, seq, d), y_s.reshape(bd, dec_t, d),
            k_p.reshape(1, batch, seq, heads_a, LANES), v_p.reshape(1, batch, seq, heads_a, LANES),
            k_s.reshape(1, bd, dec_t, heads_a, LANES), v_s.reshape(1, bd, dec_t, heads_a, LANES),
            ssm_p[None], conv_p[None], ssm_s[None], conv_s[None])
```
